```python
import jax, jax.numpy as jnp
from jax import lax
import numpy as np

D_MODEL = 1024
BATCH = 8
SEQ = 2048
DEPTH = 1

CHUNK = 64
Q_BLOCK = 128
EPS = 1e-6

GLA_HEADS = 4
GLA_DK = 64
GLA_DV = 128
GLA_RANK = 16
GLA_TAU = 16.0
SB_HEADS = 8
SB_DH = 64
MEM_LEN = 256
MEM_HEADS = 4
MEM_DH = D_MODEL // MEM_HEADS
PEER_HEADS = 8
PEER_NKEYS = 128
PEER_N = PEER_NKEYS * PEER_NKEYS
PEER_DQ = 256
PEER_TOPK = 16
PEER_TOK_BLOCK = 128

GLA_QK = GLA_HEADS * GLA_DK
GLA_V = GLA_HEADS * GLA_DV
SB_W = SB_HEADS * SB_DH
MIX_W = GLA_V + SB_W
_SPLIT_SIZES = (GLA_QK, GLA_QK, GLA_V, GLA_V, GLA_RANK, SB_W, SB_W, SB_W)
IN_COLS = sum(_SPLIT_SIZES)

kernel_name = "hybrid_gla_stickbreak_peer_block"


def _rmsnorm(x, g):
    xf = x.astype(jnp.float32)
    y = xf * lax.rsqrt(jnp.mean(xf * xf, axis=-1, keepdims=True) + EPS)
    return (y * g.astype(jnp.float32)).astype(x.dtype)


def _gla(q, k, v, g):
    B_, S_, H, dk = q.shape
    dv = v.shape[-1]
    n = S_ // CHUNK

    def to_chunks(t):
        return t.reshape(B_, n, CHUNK, H, t.shape[-1]).transpose(1, 0, 3, 2, 4)

    qc, kc, vc, gc = (to_chunks(t) for t in (q * (dk ** -0.5), k, v, g))
    causal = jnp.tril(jnp.ones((CHUNK, CHUNK), dtype=bool))

    def step(state, inp):
        q_, k_, v_, g_ = inp
        b = jnp.cumsum(g_.astype(jnp.float32), axis=-2)
        b_last = b[..., -1:, :]
        o_inter = jnp.einsum('bhcd,bhde->bhce', q_ * jnp.exp(b), state)
        diff = b[:, :, :, None, :] - b[:, :, None, :, :]
        decay = jnp.where(causal[:, :, None], jnp.exp(jnp.minimum(diff, 0.0)), 0.0)
        attn = jnp.einsum('bhtd,bhsd,bhtsd->bhts', q_, k_, decay)
        o = o_inter + jnp.einsum('bhts,bhse->bhte', attn, v_)
        new_state = state * jnp.exp(b_last).swapaxes(-1, -2) + jnp.einsum(
            'bhsd,bhse->bhde', k_ * jnp.exp(b_last - b), v_)
        return new_state, o

    state0 = jnp.zeros((B_, H, dk, dv), jnp.float32)
    _, o = lax.scan(step, state0, (qc, kc, vc, gc))
    return o.transpose(1, 0, 3, 2, 4).reshape(B_, S_, H, dv)


def _stick_breaking(q, k, v):
    B_, S_, H, d = q.shape
    qh = q.transpose(0, 2, 1, 3).astype(jnp.float32) * (d ** -0.5)
    kh = k.transpose(0, 2, 1, 3).astype(jnp.float32)
    vh = v.transpose(0, 2, 1, 3).astype(jnp.float32)
    outs = []
    for i in range(S_ // Q_BLOCK):
        start, end = i * Q_BLOCK, (i + 1) * Q_BLOCK
        z = jnp.einsum('bhtd,bhsd->bhts', qh[:, :, start:end], kh[:, :, :end])
        mask = jnp.arange(end)[None, :] < jnp.arange(start, end)[:, None]
        log_1m = jnp.where(mask, jax.nn.log_sigmoid(-z), 0.0)
        suffix = lax.cumsum(log_1m, axis=3, reverse=True) - log_1m
        a = jnp.where(mask, jnp.exp(jax.nn.log_sigmoid(z) + suffix), 0.0)
        outs.append(jnp.einsum('bhts,bhsd->bhtd', a, vh[:, :, :end]))
    o = jnp.concatenate(outs, axis=2)
    return o.transpose(0, 2, 1, 3)


def _mem_xattn(hn, memn, w_q, w_kv, w_o):
    B_, S_, D = hn.shape
    M = memn.shape[1]
    q = (hn @ w_q).reshape(B_, S_, MEM_HEADS, MEM_DH)
    k, v = jnp.split(memn @ w_kv, 2, axis=-1)
    k = k.reshape(B_, M, MEM_HEADS, MEM_DH)
    v = v.reshape(B_, M, MEM_HEADS, MEM_DH)
    s = jnp.einsum('bshd,bmhd->bhsm', q, k).astype(jnp.float32) * (MEM_DH ** -0.5)
    p = jax.nn.softmax(s, axis=-1).astype(v.dtype)
    o = jnp.einsum('bhsm,bmhd->bshd', p, v).reshape(B_, S_, D)
    return o @ w_o


def _peer(xn, w_pq, sub_keys, u, v_emb):
    B_, S_, D = xn.shape
    T = B_ * S_
    H, K = PEER_HEADS, PEER_TOPK
    xt = xn.reshape(T, D)
    q = (xt @ w_pq).reshape(T, H, 2, PEER_DQ // 2)
    s = jnp.einsum('thpd,hpnd->thpn', q, sub_keys).astype(jnp.float32)
    top_s, top_i = lax.top_k(s, K)
    cand_s = top_s[:, :, 0, :, None] + top_s[:, :, 1, None, :]
    cand_i = top_i[:, :, 0, :, None] * PEER_NKEYS + top_i[:, :, 1, None, :]
    best_s, best_j = lax.top_k(cand_s.reshape(T, H, K * K), K)
    expert = jnp.take_along_axis(cand_i.reshape(T, H, K * K), best_j, axis=-1)
    gate = jax.nn.softmax(best_s, axis=-1)

    nb = T // PEER_TOK_BLOCK

    def block(args):
        xb, eb, gb = args
        ub = u[eb]
        act = jax.nn.gelu(jnp.einsum('thkd,td->thk', ub, xb).astype(jnp.float32),
                          approximate=False)
        w = (gb * act).astype(xb.dtype)
        return jnp.einsum('thk,thkd->td', w, v_emb[eb])

    out = lax.map(block, (xt.reshape(nb, PEER_TOK_BLOCK, D),
                          expert.reshape(nb, PEER_TOK_BLOCK, H, K),
                          gate.reshape(nb, PEER_TOK_BLOCK, H, K)))
    return out.reshape(B_, S_, D)


def setup_inputs(seed: int = 0) -> dict:
    key = jax.random.key(seed)
    ks = jax.random.split(key, 24)
    nrm = jax.random.normal
    f32 = jnp.float32
    L, D = DEPTH, D_MODEL

    def gain(k, shape):
        return 1.0 + 0.02 * nrm(k, shape, f32)

    return {
        "x": nrm(ks[0], (BATCH, SEQ, D), f32),
        "mem": nrm(ks[1], (BATCH, MEM_LEN, D), f32),
        "g_mix": gain(ks[2], (L, D)),
        "w_in": nrm(ks[3], (L, D, IN_COLS), f32) * D ** -0.5,
        "w_alpha_up": nrm(ks[4], (L, GLA_RANK, GLA_QK), f32) * GLA_RANK ** -0.5,
        "b_alpha": 0.1 * nrm(ks[5], (L, GLA_QK), f32),
        "g_gla_out": gain(ks[6], (L, GLA_DV)),
        "w_out": nrm(ks[7], (L, MIX_W, D), f32) * MIX_W ** -0.5,
        "g_mem_q": gain(ks[8], (L, D)),
        "g_mem_kv": gain(ks[9], (L, D)),
        "w_mq": nrm(ks[10], (L, D, D), f32) * D ** -0.5,
        "w_mkv": nrm(ks[11], (L, D, 2 * D), f32) * D ** -0.5,
        "w_mo": nrm(ks[12], (L, D, D), f32) * D ** -0.5,
        "g_ffn": gain(ks[13], (L, D)),
        "w_pq": nrm(ks[14], (L, D, PEER_HEADS * PEER_DQ), f32) * D ** -0.5,
        "sub_keys": nrm(ks[15], (L, PEER_HEADS, 2, PEER_NKEYS, PEER_DQ // 2), f32) * (PEER_DQ // 2) ** -0.5,
        "peer_u": nrm(ks[16], (L, PEER_N, D), f32) * D ** -0.5,
        "peer_v": nrm(ks[17], (L, PEER_N, D), f32) * (PEER_HEADS * PEER_TOPK) ** -0.5,
        "g_final": gain(ks[18], (D,)),
    }


def reference(x, mem, g_mix, w_in, w_alpha_up, b_alpha, g_gla_out, w_out,
              g_mem_q, g_mem_kv, w_mq, w_mkv, w_mo, g_ffn, w_pq, sub_keys,
              peer_u, peer_v, g_final):
    B_, S_, D = x.shape
    split_idx = np.cumsum(_SPLIT_SIZES)[:-1].tolist()
    h = x
    for i in range(DEPTH):
        hn = _rmsnorm(h, g_mix[i])
        proj = hn @ w_in[i]
        qg, kg, vg, og, alr, qs, ksb, vs = jnp.split(proj, split_idx, axis=-1)
        log_alpha = jax.nn.log_sigmoid(
            (alr @ w_alpha_up[i] + b_alpha[i]).astype(jnp.float32)) / GLA_TAU
        o_gla = _gla(qg.reshape(B_, S_, GLA_HEADS, GLA_DK),
                     kg.reshape(B_, S_, GLA_HEADS, GLA_DK),
                     vg.reshape(B_, S_, GLA_HEADS, GLA_DV),
                     log_alpha.reshape(B_, S_, GLA_HEADS, GLA_DK))
        o_gla = _rmsnorm(o_gla, g_gla_out[i]).reshape(B_, S_, GLA_V)
        o_gla = o_gla * jax.nn.silu(og.astype(jnp.float32))
        o_sb = _stick_breaking(qs.reshape(B_, S_, SB_HEADS, SB_DH),
                               ksb.reshape(B_, S_, SB_HEADS, SB_DH),
                               vs.reshape(B_, S_, SB_HEADS, SB_DH)).reshape(B_, S_, SB_W)
        mix = jnp.concatenate([o_gla, o_sb], axis=-1).astype(h.dtype)
        h = h + mix @ w_out[i]
        h = h + _mem_xattn(_rmsnorm(h, g_mem_q[i]), _rmsnorm(mem, g_mem_kv[i]),
                           w_mq[i], w_mkv[i], w_mo[i])
        h = h + _peer(_rmsnorm(h, g_ffn[i]), w_pq[i], sub_keys[i], peer_u[i], peer_v[i])
    return _rmsnorm(h, g_final)
```

```python
import functools

import numpy as np
import jax
import jax.numpy as jnp
from jax import lax
from jax.experimental import pallas as pl
from jax.experimental.pallas import tpu as pltpu

F32 = jnp.float32
BF16 = jnp.bfloat16
EPS = 1e-6

D_MODEL = 1024
CHUNK = 64
GLA_HEADS, GLA_DK, GLA_DV, GLA_RANK, GLA_TAU = 4, 64, 128, 16, 16.0
GLA_QK = GLA_HEADS * GLA_DK
GLA_V = GLA_HEADS * GLA_DV
SB_HEADS, SB_DH = 8, 64
SB_W = SB_HEADS * SB_DH
MEM_HEADS = 4
MEM_DH = D_MODEL // MEM_HEADS
PEER_HEADS, PEER_NKEYS, PEER_DQ, PEER_TOPK = 8, 128, 256, 16
PEER_N = PEER_NKEYS * PEER_NKEYS
PEER_HK = PEER_HEADS * PEER_TOPK
LANES = 128
GLA_LEVELS = 6
VMEM_LIMIT = 56 * 1024 * 1024


def _nt(a, b):
    return lax.dot_general(a, b, (((1,), (1,)), ((), ())), preferred_element_type=F32)


def _mm(a, b):
    return jnp.dot(a, b, preferred_element_type=F32)


def _split(x):
    hi = x.astype(BF16)
    lo = (x - hi.astype(F32)).astype(BF16)
    return hi, lo


def _rms(x, g):
    return x * lax.rsqrt(jnp.mean(x * x, axis=-1, keepdims=True) + EPS) * g


def _log_sigmoid(x):
    return jnp.minimum(x, 0.0) - jnp.log1p(jnp.exp(-jnp.abs(x)))


def _params(*sem):
    return pltpu.CompilerParams(dimension_semantics=sem, vmem_limit_bytes=VMEM_LIMIT)


def _mm3(a, b):
    a_hi, a_lo = _split(a)
    b_hi, b_lo = _split(b)
    return _mm(a_hi, b_hi) + _mm(a_lo, b_hi) + _mm(a_hi, b_lo)


def _inproj_kernel(x_ref, g_ref, w_ref, wa_ref, wup_ref, ba_ref, pg_ref, la_ref, ps_ref):
    hn32 = _rms(x_ref[...], g_ref[...])
    y = _mm(hn32.astype(BF16), w_ref[...])
    ng = GLA_QK * 2 + GLA_V * 2
    pg_ref[:, :GLA_QK] = y[:, :GLA_QK] * (GLA_DK ** -0.5)
    pg_ref[:, GLA_QK:] = y[:, GLA_QK:ng]
    ps_ref[:, :SB_W] = (y[:, ng:ng + SB_W] * (SB_DH ** -0.5)).astype(BF16)
    ps_ref[:, SB_W:] = y[:, ng + SB_W:].astype(BF16)
    alr = _mm3(hn32, wa_ref[...])
    pre = _mm3(alr, wup_ref[...]) + ba_ref[...]
    la_ref[...] = _log_sigmoid(pre) * (1.0 / GLA_TAU)


def _inproj(x2, g_mix, w_in, w_alpha_up, b_alpha, tm):
    T = x2.shape[0]
    ng = GLA_QK * 2 + GLA_V * 2
    w_cat = jnp.concatenate([w_in[:, :ng], w_in[:, ng + GLA_RANK:]], axis=1).astype(BF16)
    w_alr = jnp.pad(w_in[:, ng:ng + GLA_RANK], ((0, 0), (0, LANES - GLA_RANK)))
    wup = jnp.pad(w_alpha_up, ((0, LANES - GLA_RANK), (0, 0)))
    nw = w_cat.shape[1]
    return pl.pallas_call(
        _inproj_kernel,
        grid=(T // tm,),
        in_specs=[
            pl.BlockSpec((tm, D_MODEL), lambda i: (i, 0)),
            pl.BlockSpec((1, D_MODEL), lambda i: (0, 0)),
            pl.BlockSpec((D_MODEL, nw), lambda i: (0, 0)),
            pl.BlockSpec((D_MODEL, LANES), lambda i: (0, 0)),
            pl.BlockSpec((LANES, GLA_QK), lambda i: (0, 0)),
            pl.BlockSpec((1, GLA_QK), lambda i: (0, 0)),
        ],
        out_specs=[
            pl.BlockSpec((tm, ng), lambda i: (i, 0)),
            pl.BlockSpec((tm, GLA_QK), lambda i: (i, 0)),
            pl.BlockSpec((tm, 3 * SB_W), lambda i: (i, 0)),
        ],
        out_shape=[
            jax.ShapeDtypeStruct((T, ng), F32),
            jax.ShapeDtypeStruct((T, GLA_QK), F32),
            jax.ShapeDtypeStruct((T, 3 * SB_W), BF16),
        ],
        compiler_params=_params("parallel"),
        name="inproj",
    )(x2, g_mix.reshape(1, D_MODEL), w_cat, w_alr, wup, b_alpha.reshape(1, GLA_QK))


def _gla_constants():
    C = CHUNK
    t = np.arange(C)[:, None]
    j = np.arange(C)[None, :]
    mats = [(j <= t), (j > t)]
    masks = []
    for l in range(GLA_LEVELS):
        low = (1 << l) - 1
        mats.append((j > (t & ~low)) & (j <= t))
        mats.append((j > t) & (j <= np.minimum((t | low) + 1, C - 1)))
        s = j
        masks.append(((t >> (l + 1)) == (s >> (l + 1))) & (((t >> l) & 1) == 1) & (((s >> l) & 1) == 0))
    masks.append(t == j)
    m = np.concatenate(mats, axis=0).astype(np.float32)
    return jnp.asarray(m, BF16), jnp.asarray(np.stack(masks).astype(np.float32))


def _gla_kernel(q_ref, k_ref, v_ref, og_ref, la_ref, m_ref, mask_ref, gout_ref, o_ref, state_ref):
    C = CHUNK

    @pl.when(pl.program_id(1) == 0)
    def _():
        state_ref[...] = jnp.zeros_like(state_ref)

    q = q_ref[0]
    k = k_ref[0]
    g_hi, g_lo = _split(la_ref[0])
    m = m_ref[...]
    e = jnp.exp(_mm(m, g_hi) + _mm(m, g_lo))
    lane = lax.broadcasted_iota(jnp.int32, (C, GLA_QK), 1)

    qd = q * e[0:C]
    kd = k * e[C:2 * C]
    q_lv = [(q * e[(2 + 2 * l) * C:(3 + 2 * l) * C]) for l in range(GLA_LEVELS)] + [q]
    k_lv = [(k * e[(3 + 2 * l) * C:(4 + 2 * l) * C]).astype(BF16) for l in range(GLA_LEVELS)] + [k.astype(BF16)]

    state = state_ref[...]
    state_b = state.astype(BF16)
    dec = jnp.transpose(jnp.broadcast_to(e[C - 1:C], (LANES, GLA_QK)))
    kdt = jnp.transpose(kd).astype(BF16)
    gout = gout_ref[...]

    new_rows = []
    for h in range(GLA_HEADS):
        hm = (lane >= h * GLA_DK) & (lane < (h + 1) * GLA_DK)
        attn = jnp.zeros((C, C), F32)
        for l in range(GLA_LEVELS + 1):
            ql = jnp.where(hm, q_lv[l], 0.0).astype(BF16)
            attn = attn + mask_ref[l] * _nt(ql, k_lv[l])
        v_h = v_ref[0, :, h * GLA_DV:(h + 1) * GLA_DV].astype(BF16)
        o = _mm(jnp.where(hm, qd, 0.0).astype(BF16), state_b) + _mm(attn.astype(BF16), v_h)
        o = _rms(o, gout)
        og = og_ref[0, :, h * GLA_DV:(h + 1) * GLA_DV]
        o_ref[0, :, h * GLA_DV:(h + 1) * GLA_DV] = (o * (og * jax.nn.sigmoid(og))).astype(BF16)
        new_rows.append(_mm(kdt[h * GLA_DK:(h + 1) * GLA_DK], v_h))
    state_ref[...] = state * dec + jnp.concatenate(new_rows, axis=0)


def _gla(pg3, la3, g_gla_out):
    B, S, _ = pg3.shape
    mstack, masks = _gla_constants()
    C = CHUNK
    return pl.pallas_call(
        _gla_kernel,
        grid=(B, S // C),
        in_specs=[
            pl.BlockSpec((1, C, GLA_QK), lambda b, c: (b, c, 0)),
            pl.BlockSpec((1, C, GLA_QK), lambda b, c: (b, c, 1)),
            pl.BlockSpec((1, C, GLA_V), lambda b, c: (b, c, 1)),
            pl.BlockSpec((1, C, GLA_V), lambda b, c: (b, c, 2)),
            pl.BlockSpec((1, C, GLA_QK), lambda b, c: (b, c, 0)),
            pl.BlockSpec(mstack.shape, lambda b, c: (0, 0)),
            pl.BlockSpec(masks.shape, lambda b, c: (0, 0, 0)),
            pl.BlockSpec((1, GLA_DV), lambda b, c: (0, 0)),
        ],
        out_specs=pl.BlockSpec((1, C, GLA_V), lambda b, c: (b, c, 0)),
        out_shape=jax.ShapeDtypeStruct((B, S, GLA_V), BF16),
        scratch_shapes=[pltpu.VMEM((GLA_QK, GLA_DV), F32)],
        compiler_params=_params("parallel", "arbitrary"),
        name="gla",
    )(pg3, pg3, pg3, pg3, la3, mstack, masks, g_gla_out.reshape(1, GLA_DV))


SB_BLK = 128


def _sb_kernel(q_ref, k_ref, v_ref, u_ref, o_ref):
    tq = SB_BLK
    qi = pl.program_id(2)
    q = q_ref[0]
    u = u_ref[...]
    lane = lax.broadcasted_iota(jnp.int32, (tq, LANES), 1)
    row = lax.broadcasted_iota(jnp.int32, (tq, tq), 0)
    col = lax.broadcasted_iota(jnp.int32, (tq, tq), 1)
    causal = col < row

    def block(qm, kb, acc, run, masked):
        start = pl.multiple_of(kb * tq, tq)
        kblk = k_ref[0, pl.ds(start, tq), :]
        vblk = v_ref[0, pl.ds(start, tq), :]
        z = _nt(qm, kblk)
        sp = jnp.log1p(jnp.exp(-jnp.abs(z)))
        ls = jnp.minimum(z, 0.0) - sp
        l1m = -jnp.maximum(z, 0.0) - sp
        if masked:
            l1m = jnp.where(causal, l1m, 0.0)
        hi, lo = _split(l1m)
        suf = _mm(hi, u) + _mm(lo, u)
        a = jnp.exp(ls + suf + run)
        if masked:
            a = jnp.where(causal, a, 0.0)
        acc = acc + _mm(a.astype(BF16), vblk)
        run = run + suf[:, 0:1] + l1m[:, 0:1]
        return acc, run

    outs = []
    for head in range(2):
        hm = (lane >= head * SB_DH) & (lane < (head + 1) * SB_DH)
        qm = jnp.where(hm, q, jnp.zeros_like(q))
        acc, run = block(qm, qi, jnp.zeros((tq, LANES), F32), jnp.zeros((tq, 1), F32), True)

        def body(i, carry, qm=qm):
            return block(qm, qi - 1 - i, carry[0], carry[1], False)

        acc, run = lax.fori_loop(0, qi, body, (acc, run))
        outs.append(acc)
    o_ref[0] = jnp.where(lane < SB_DH, outs[0], outs[1]).astype(BF16)


def _stick_breaking(ps3):
    B, S, _ = ps3.shape
    tq = SB_BLK
    pairs = SB_HEADS // 2
    u = jnp.asarray(np.arange(tq)[:, None] > np.arange(tq)[None, :], BF16)
    return pl.pallas_call(
        _sb_kernel,
        grid=(B, pairs, S // tq),
        in_specs=[
            pl.BlockSpec((1, tq, LANES), lambda b, p, i: (b, i, p)),
            pl.BlockSpec((1, S, LANES), lambda b, p, i: (b, 0, pairs + p)),
            pl.BlockSpec((1, S, LANES), lambda b, p, i: (b, 0, 2 * pairs + p)),
            pl.BlockSpec((tq, tq), lambda b, p, i: (0, 0)),
        ],
        out_specs=pl.BlockSpec((1, tq, LANES), lambda b, p, i: (b, i, p)),
        out_shape=jax.ShapeDtypeStruct((B, S, SB_W), BF16),
        compiler_params=_params("parallel", "parallel", "arbitrary"),
        name="stickbreak",
    )(ps3, ps3, ps3, u)


def _outproj_kernel(x_ref, og_ref, os_ref, wo_ref, g_ref, wq_ref, h_ref, q_ref):
    h = x_ref[...] + _mm(og_ref[...], wo_ref[:GLA_V]) + _mm(os_ref[...], wo_ref[GLA_V:])
    h_ref[...] = h
    q_ref[...] = (_mm(_rms(h, g_ref[...]).astype(BF16), wq_ref[...]) * (MEM_DH ** -0.5)).astype(BF16)


def _outproj(x2, o_gla, o_sb, w_out, g_mem_q, w_mq, tm):
    T = x2.shape[0]
    row = lambda w: pl.BlockSpec((tm, w), lambda i: (i, 0))
    full = lambda r, c: pl.BlockSpec((r, c), lambda i: (0, 0))
    return pl.pallas_call(
        _outproj_kernel,
        grid=(T // tm,),
        in_specs=[row(D_MODEL), row(GLA_V), row(SB_W), full(GLA_V + SB_W, D_MODEL),
                  full(1, D_MODEL), full(D_MODEL, D_MODEL)],
        out_specs=[row(D_MODEL), row(D_MODEL)],
        out_shape=[jax.ShapeDtypeStruct((T, D_MODEL), F32), jax.ShapeDtypeStruct((T, D_MODEL), BF16)],
        compiler_params=_params("parallel"),
        name="outproj",
    )(x2, o_gla, o_sb, w_out.astype(BF16), g_mem_q.reshape(1, D_MODEL), w_mq.astype(BF16))


def _memkv_kernel(m_ref, g_ref, w_ref, kv_ref):
    kv_ref[...] = _mm(_rms(m_ref[...], g_ref[...]).astype(BF16), w_ref[...]).astype(BF16)


def _memkv(mem2, g_mem_kv, w_mkv, tm):
    R = mem2.shape[0]
    return pl.pallas_call(
        _memkv_kernel,
        grid=(R // tm,),
        in_specs=[pl.BlockSpec((tm, D_MODEL), lambda i: (i, 0)),
                  pl.BlockSpec((1, D_MODEL), lambda i: (0, 0)),
                  pl.BlockSpec((D_MODEL, 2 * D_MODEL), lambda i: (0, 0))],
        out_specs=pl.BlockSpec((tm, 2 * D_MODEL), lambda i: (i, 0)),
        out_shape=jax.ShapeDtypeStruct((R, 2 * D_MODEL), BF16),
        compiler_params=_params("parallel"),
        name="memkv",
    )(mem2, g_mem_kv.reshape(1, D_MODEL), w_mkv.astype(BF16))


def _memattn_kernel(h_ref, q_ref, k_ref, v_ref, wo_ref, g_ref, wpq_ref, h2_ref, xn_ref, qp_ref):
    q = q_ref[0]
    outs = []
    for hd in range(MEM_HEADS):
        sl = slice(hd * MEM_DH, (hd + 1) * MEM_DH)
        s = _nt(q[:, sl], k_ref[0, :, sl])
        p = jnp.exp(s - jnp.max(s, axis=-1, keepdims=True))
        p = p / jnp.sum(p, axis=-1, keepdims=True)
        outs.append(_mm(p.astype(BF16), v_ref[0, :, sl]).astype(BF16))
    o = jnp.concatenate(outs, axis=-1)
    h2 = h_ref[0] + _mm(o, wo_ref[...])
    h2_ref[0] = h2
    xn = _rms(h2, g_ref[...]).astype(BF16)
    xn_ref[0] = xn
    qp = _mm(xn, wpq_ref[...]).astype(BF16)
    for hp in range(2 * PEER_HEADS):
        qp_ref[hp, 0] = qp[:, hp * LANES:(hp + 1) * LANES]


def _memattn(h1_3, qm_3, kv_3, w_mo, g_ffn, w_pq, tm):
    B, S, _ = h1_3.shape
    M = kv_3.shape[1]
    nq = PEER_HEADS * PEER_DQ
    row = lambda w: pl.BlockSpec((1, tm, w), lambda b, i: (b, i, 0))
    full = lambda r, c: pl.BlockSpec((r, c), lambda b, i: (0, 0))
    return pl.pallas_call(
        _memattn_kernel,
        grid=(B, S // tm),
        in_specs=[row(D_MODEL), row(D_MODEL),
                  pl.BlockSpec((1, M, D_MODEL), lambda b, i: (b, 0, 0)),
                  pl.BlockSpec((1, M, D_MODEL), lambda b, i: (b, 0, 1)),
                  full(D_MODEL, D_MODEL), full(1, D_MODEL), full(D_MODEL, nq)],
        out_specs=[row(D_MODEL), row(D_MODEL),
                   pl.BlockSpec((2 * PEER_HEADS, 1, tm, LANES), lambda b, i: (0, b, i, 0))],
        out_shape=[jax.ShapeDtypeStruct((B, S, D_MODEL), F32),
                   jax.ShapeDtypeStruct((B, S, D_MODEL), BF16),
                   jax.ShapeDtypeStruct((2 * PEER_HEADS, B, S, LANES), BF16)],
        compiler_params=_params("parallel", "parallel"),
        name="memattn",
    )(h1_3, qm_3, kv_3, kv_3, w_mo.astype(BF16), g_ffn.reshape(1, D_MODEL), w_pq.astype(BF16))


def _extract_topk(vals, n_rows, tm, out_val_ref, out_idx_ref, base):
    iota = lax.broadcasted_iota(jnp.int32, (n_rows, tm), 0).astype(F32)

    def body(kk, s):
        m = jnp.max(s, axis=0, keepdims=True)
        idx = jnp.min(jnp.where(s == m, iota, float(n_rows)), axis=0, keepdims=True)
        out_val_ref[pl.ds(base + kk, 1), :] = m
        out_idx_ref[pl.ds(base + kk, 1), :] = idx
        return jnp.where(iota == idx, -jnp.inf, s)

    lax.fori_loop(0, PEER_TOPK, body, vals)


def _select_rows(idx, table):
    out = jnp.zeros(idx.shape, table.dtype)
    for a in range(PEER_TOPK):
        out = jnp.where(idx == a, table[a:a + 1, :], out)
    return out


def _route_kernel(qp_ref, keys_ref, i1_ref, i2_ref, gate_ref, ts_ref, ti_ref, bs_ref, bj_ref):
    tm = qp_ref.shape[1]
    K = PEER_TOPK

    def stage1(hp, carry):
        s = _nt(keys_ref[hp], qp_ref[hp])
        _extract_topk(s, PEER_NKEYS, tm, ts_ref, ti_ref, hp * K)
        return carry

    lax.fori_loop(0, 2 * PEER_HEADS, stage1, 0)

    def stage2(h, carry):
        r0 = pl.multiple_of(2 * h * K, K)
        r1 = pl.multiple_of((2 * h + 1) * K, K)
        s0 = ts_ref[pl.ds(r0, K), :]
        s1 = ts_ref[pl.ds(r1, K), :]
        cand = jnp.concatenate([s0[a:a + 1, :] + s1 for a in range(K)], axis=0)
        _extract_topk(cand, K * K, tm, bs_ref, bj_ref, 0)
        best = bs_ref[...]
        j = bj_ref[...].astype(jnp.int32)
        ex = jnp.exp(best - best[0:1, :])
        out = pl.multiple_of(h * K, K)
        gate_ref[pl.ds(out, K), :] = ex / jnp.sum(ex, axis=0, keepdims=True)
        i1_ref[pl.ds(out, K), :] = _select_rows(j >> 4, ti_ref[pl.ds(r0, K), :])
        i2_ref[pl.ds(out, K), :] = _select_rows(j & (K - 1), ti_ref[pl.ds(r1, K), :])
        return carry

    lax.fori_loop(0, PEER_HEADS, stage2, 0)


def _route(qp3, sub_keys, tm):
    T = qp3.shape[1]
    keys = sub_keys.reshape(2 * PEER_HEADS, PEER_NKEYS, PEER_DQ // 2).astype(BF16)
    out = lambda: pl.BlockSpec((PEER_HK, tm), lambda i: (0, i))
    stage1_rows = 2 * PEER_HEADS * PEER_TOPK
    return pl.pallas_call(
        _route_kernel,
        grid=(T // tm,),
        in_specs=[pl.BlockSpec((2 * PEER_HEADS, tm, LANES), lambda i: (0, i, 0)),
                  pl.BlockSpec(keys.shape, lambda i: (0, 0, 0))],
        out_specs=[out(), out(), out()],
        out_shape=[jax.ShapeDtypeStruct((PEER_HK, T), F32)] * 3,
        scratch_shapes=[pltpu.VMEM((stage1_rows, tm), F32),
                        pltpu.VMEM((stage1_rows, tm), F32),
                        pltpu.VMEM((PEER_TOPK, tm), F32),
                        pltpu.VMEM((PEER_TOPK, tm), F32)],
        compiler_params=_params("parallel"),
        name="peer_route",
    )(qp3, keys)


def _gates_kernel(i1_ref, i2_ref, gate_ref, g_ref, a_ref, b_ref, w_ref):
    tg = g_ref.shape[0]
    a_ref[...] = jnp.transpose(i1_ref[...])
    b_ref[...] = jnp.transpose(i2_ref[...])
    w_ref[...] = jnp.transpose(gate_ref[...])
    sub = lax.broadcasted_iota(jnp.int32, (PEER_NKEYS, PEER_HK), 0).astype(F32)

    def body(t, carry):
        a = a_ref[pl.ds(t, 1), :]
        b = b_ref[pl.ds(t, 1), :]
        w = w_ref[pl.ds(t, 1), :]
        oh1 = jnp.where(sub == a, 1.0, 0.0).astype(BF16)
        oh2 = jnp.where(sub == b, w, 0.0).astype(BF16)
        g_ref[t] = _nt(oh1, oh2).astype(BF16)
        return carry

    lax.fori_loop(0, tg, body, 0)


def _gates(i1, i2, gate, tg):
    T = i1.shape[1]
    blk = lambda: pl.BlockSpec((PEER_HK, tg), lambda i: (0, i))
    return pl.pallas_call(
        _gates_kernel,
        grid=(T // tg,),
        in_specs=[blk(), blk(), blk()],
        out_specs=pl.BlockSpec((tg, PEER_NKEYS, PEER_NKEYS), lambda i: (i, 0, 0)),
        out_shape=jax.ShapeDtypeStruct((T, PEER_NKEYS, PEER_NKEYS), BF16),
        scratch_shapes=[pltpu.VMEM((tg, PEER_HK), F32)] * 3,
        compiler_params=_params("parallel"),
        name="peer_gates",
    )(i1, i2, gate)


def _peer_kernel(xn_ref, ut_ref, g_ref, v_ref, h_ref, gf_ref, o_ref, acc_ref):
    j = pl.program_id(1)

    @pl.when(j == 0)
    def _():
        acc_ref[...] = jnp.zeros_like(acc_ref)

    p = _mm(xn_ref[...], ut_ref[...])
    act = 0.5 * p * (1.0 + lax.erf(p * (2.0 ** -0.5)))
    w = (g_ref[...].astype(F32) * act).astype(BF16)
    acc_ref[...] += _mm(w, v_ref[...])

    @pl.when(j == pl.num_programs(1) - 1)
    def _():
        o_ref[...] = _rms(h_ref[...] + acc_ref[...], gf_ref[...])


def _peer(xn2, g2, peer_u, peer_v, h2, g_final, tb, nb):
    T = xn2.shape[0]
    ut = peer_u.T.astype(BF16)
    vb = peer_v.astype(BF16)
    return pl.pallas_call(
        _peer_kernel,
        grid=(T // tb, PEER_N // nb),
        in_specs=[pl.BlockSpec((tb, D_MODEL), lambda i, j: (i, 0)),
                  pl.BlockSpec((D_MODEL, nb), lambda i, j: (0, j)),
                  pl.BlockSpec((tb, nb), lambda i, j: (i, j)),
                  pl.BlockSpec((nb, D_MODEL), lambda i, j: (j, 0)),
                  pl.BlockSpec((tb, D_MODEL), lambda i, j: (i, 0)),
                  pl.BlockSpec((1, D_MODEL), lambda i, j: (0, 0))],
        out_specs=pl.BlockSpec((tb, D_MODEL), lambda i, j: (i, 0)),
        out_shape=jax.ShapeDtypeStruct((T, D_MODEL), F32),
        scratch_shapes=[pltpu.VMEM((tb, D_MODEL), F32)],
        compiler_params=_params("parallel", "arbitrary"),
        name="peer_dense",
    )(xn2, ut, g2, vb, h2, g_final.reshape(1, D_MODEL))


def _tile(n, want):
    t = min(n, want)
    assert n % t == 0, (n, t)
    return t


def kernel(x, mem, g_mix, w_in, w_alpha_up, b_alpha, g_gla_out, w_out, g_mem_q, g_mem_kv, w_mq, w_mkv, w_mo,
           g_ffn, w_pq, sub_keys, peer_u, peer_v, g_final):
    B, S, D = x.shape
    T = B * S
    assert D == D_MODEL and S % SB_BLK == 0 and g_mix.shape[0] == 1
    x2 = x.reshape(T, D)

    pg, la, ps = _inproj(x2, g_mix[0], w_in[0], w_alpha_up[0], b_alpha[0], _tile(T, 256))
    o_gla = _gla(pg.reshape(B, S, -1), la.reshape(B, S, -1), g_gla_out[0])
    o_sb = _stick_breaking(ps.reshape(B, S, -1))
    h1, qm = _outproj(x2, o_gla.reshape(T, -1), o_sb.reshape(T, -1), w_out[0], g_mem_q[0], w_mq[0], _tile(T, 512))
    M = mem.shape[1]
    kv = _memkv(mem.reshape(B * M, D), g_mem_kv[0], w_mkv[0], _tile(B * M, 256))
    h2, xn, qp = _memattn(h1.reshape(B, S, D), qm.reshape(B, S, D), kv.reshape(B, M, 2 * D),
                          w_mo[0], g_ffn[0], w_pq[0], _tile(S, 256))
    i1, i2, gate = _route(qp.reshape(2 * PEER_HEADS, T, LANES), sub_keys[0], _tile(T, 128))
    gmat = _gates(i1, i2, gate, _tile(T, 128))
    out = _peer(xn.reshape(T, D), gmat.reshape(T, PEER_N), peer_u[0], peer_v[0], h2.reshape(T, D), g_final,
                _tile(T, 1024), 512)
    return out.reshape(B, S, D)
```

```python
import functools

import numpy as np
import jax
import jax.numpy as jnp
from jax import lax
from jax.experimental import pallas as pl
from jax.experimental.pallas import tpu as pltpu

F32 = jnp.float32
BF16 = jnp.bfloat16
EPS = 1e-6

D_MODEL = 1024
CHUNK = 64
GLA_HEADS, GLA_DK, GLA_DV, GLA_RANK, GLA_TAU = 4, 64, 128, 16, 16.0
GLA_QK = GLA_HEADS * GLA_DK
GLA_V = GLA_HEADS * GLA_DV
SB_HEADS, SB_DH = 8, 64
SB_W = SB_HEADS * SB_DH
MEM_HEADS = 4
MEM_DH = D_MODEL // MEM_HEADS
PEER_HEADS, PEER_NKEYS, PEER_DQ, PEER_TOPK = 8, 128, 256, 16
PEER_N = PEER_NKEYS * PEER_NKEYS
PEER_HK = PEER_HEADS * PEER_TOPK
LANES = 128
GLA_LEVELS = 6
VMEM_LIMIT = 56 * 1024 * 1024


def _nt(a, b):
    return lax.dot_general(a, b, (((1,), (1,)), ((), ())), preferred_element_type=F32)


def _mm(a, b):
    return jnp.dot(a, b, preferred_element_type=F32)


def _split(x):
    hi = x.astype(BF16)
    lo = (x - hi.astype(F32)).astype(BF16)
    return hi, lo


def _rms(x, g):
    return x * lax.rsqrt(jnp.mean(x * x, axis=-1, keepdims=True) + EPS) * g


def _log_sigmoid(x):
    return jnp.minimum(x, 0.0) - jnp.log1p(jnp.exp(-jnp.abs(x)))


def _params(*sem):
    return pltpu.CompilerParams(dimension_semantics=sem, vmem_limit_bytes=VMEM_LIMIT)


def _mm3(a, b):
    a_hi, a_lo = _split(a)
    b_hi, b_lo = _split(b)
    return _mm(a_hi, b_hi) + _mm(a_lo, b_hi) + _mm(a_hi, b_lo)


def _inproj_kernel(x_ref, g_ref, w_ref, wa_ref, wup_ref, ba_ref, pg_ref, la_ref, ps_ref):
    hn32 = _rms(x_ref[...], g_ref[...])
    y = _mm(hn32.astype(BF16), w_ref[...])
    ng = GLA_QK * 2 + GLA_V * 2
    pg_ref[:, :GLA_QK] = y[:, :GLA_QK] * (GLA_DK ** -0.5)
    pg_ref[:, GLA_QK:] = y[:, GLA_QK:ng]
    ps_ref[:, :SB_W] = (y[:, ng:ng + SB_W] * (SB_DH ** -0.5)).astype(BF16)
    ps_ref[:, SB_W:] = y[:, ng + SB_W:].astype(BF16)
    alr = _mm3(hn32, wa_ref[...])
    pre = _mm3(alr, wup_ref[...]) + ba_ref[...]
    la_ref[...] = _log_sigmoid(pre) * (1.0 / GLA_TAU)


def _inproj(x2, g_mix, w_in, w_alpha_up, b_alpha, tm):
    T = x2.shape[0]
    ng = GLA_QK * 2 + GLA_V * 2
    w_cat = jnp.concatenate([w_in[:, :ng], w_in[:, ng + GLA_RANK:]], axis=1).astype(BF16)
    w_alr = jnp.pad(w_in[:, ng:ng + GLA_RANK], ((0, 0), (0, LANES - GLA_RANK)))
    wup = jnp.pad(w_alpha_up, ((0, LANES - GLA_RANK), (0, 0)))
    nw = w_cat.shape[1]
    return pl.pallas_call(
        _inproj_kernel,
        grid=(T // tm,),
        in_specs=[
            pl.BlockSpec((tm, D_MODEL), lambda i: (i, 0)),
            pl.BlockSpec((1, D_MODEL), lambda i: (0, 0)),
            pl.BlockSpec((D_MODEL, nw), lambda i: (0, 0)),
            pl.BlockSpec((D_MODEL, LANES), lambda i: (0, 0)),
            pl.BlockSpec((LANES, GLA_QK), lambda i: (0, 0)),
            pl.BlockSpec((1, GLA_QK), lambda i: (0, 0)),
        ],
        out_specs=[
            pl.BlockSpec((tm, ng), lambda i: (i, 0)),
            pl.BlockSpec((tm, GLA_QK), lambda i: (i, 0)),
            pl.BlockSpec((tm, 3 * SB_W), lambda i: (i, 0)),
        ],
        out_shape=[
            jax.ShapeDtypeStruct((T, ng), F32),
            jax.ShapeDtypeStruct((T, GLA_QK), F32),
            jax.ShapeDtypeStruct((T, 3 * SB_W), BF16),
        ],
        compiler_params=_params("parallel"),
        name="inproj",
    )(x2, g_mix.reshape(1, D_MODEL), w_cat, w_alr, wup, b_alpha.reshape(1, GLA_QK))


def _gla_constants():
    C = CHUNK
    t = np.arange(C)[:, None]
    j = np.arange(C)[None, :]
    mats = [(j <= t), (j > t)]
    masks = []
    for l in range(GLA_LEVELS):
        low = (1 << l) - 1
        mats.append((j > (t & ~low)) & (j <= t))
        mats.append((j > t) & (j <= np.minimum((t | low) + 1, C - 1)))
        s = j
        masks.append(((t >> (l + 1)) == (s >> (l + 1))) & (((t >> l) & 1) == 1) & (((s >> l) & 1) == 0))
    masks.append(t == j)
    m = np.concatenate(mats, axis=0).astype(np.float32)
    return jnp.asarray(m, BF16), jnp.asarray(np.stack(masks).astype(np.float32))


def _gla_kernel(q_ref, k_ref, v_ref, og_ref, la_ref, m_ref, mask_ref, gout_ref, o_ref, state_ref):
    C = CHUNK

    @pl.when(pl.program_id(1) == 0)
    def _():
        state_ref[...] = jnp.zeros_like(state_ref)

    q = q_ref[0]
    k = k_ref[0]
    g_hi, g_lo = _split(la_ref[0])
    m = m_ref[...]
    e = jnp.exp(_mm(m, g_hi) + _mm(m, g_lo))
    lane = lax.broadcasted_iota(jnp.int32, (C, GLA_QK), 1)

    qd = q * e[0:C]
    kd = k * e[C:2 * C]
    q_lv = [(q * e[(2 + 2 * l) * C:(3 + 2 * l) * C]) for l in range(GLA_LEVELS)] + [q]
    k_lv = [(k * e[(3 + 2 * l) * C:(4 + 2 * l) * C]).astype(BF16) for l in range(GLA_LEVELS)] + [k.astype(BF16)]

    state = state_ref[...]
    state_b = state.astype(BF16)
    dec = jnp.transpose(jnp.broadcast_to(e[C - 1:C], (LANES, GLA_QK)))
    kdt = jnp.transpose(kd).astype(BF16)
    gout = gout_ref[...]

    new_rows = []
    for h in range(GLA_HEADS):
        hm = (lane >= h * GLA_DK) & (lane < (h + 1) * GLA_DK)
        attn = jnp.zeros((C, C), F32)
        for l in range(GLA_LEVELS + 1):
            ql = jnp.where(hm, q_lv[l], 0.0).astype(BF16)
            attn = attn + mask_ref[l] * _nt(ql, k_lv[l])
        v_h = v_ref[0, :, h * GLA_DV:(h + 1) * GLA_DV].astype(BF16)
        o = _mm(jnp.where(hm, qd, 0.0).astype(BF16), state_b) + _mm(attn.astype(BF16), v_h)
        o = _rms(o, gout)
        og = og_ref[0, :, h * GLA_DV:(h + 1) * GLA_DV]
        o_ref[0, :, h * GLA_DV:(h + 1) * GLA_DV] = (o * (og * jax.nn.sigmoid(og))).astype(BF16)
        new_rows.append(_mm(kdt[h * GLA_DK:(h + 1) * GLA_DK], v_h))
    state_ref[...] = state * dec + jnp.concatenate(new_rows, axis=0)


def _gla(pg3, la3, g_gla_out):
    B, S, _ = pg3.shape
    mstack, masks = _gla_constants()
    C = CHUNK
    return pl.pallas_call(
        _gla_kernel,
        grid=(B, S // C),
        in_specs=[
            pl.BlockSpec((1, C, GLA_QK), lambda b, c: (b, c, 0)),
            pl.BlockSpec((1, C, GLA_QK), lambda b, c: (b, c, 1)),
            pl.BlockSpec((1, C, GLA_V), lambda b, c: (b, c, 1)),
            pl.BlockSpec((1, C, GLA_V), lambda b, c: (b, c, 2)),
            pl.BlockSpec((1, C, GLA_QK), lambda b, c: (b, c, 0)),
            pl.BlockSpec(mstack.shape, lambda b, c: (0, 0)),
            pl.BlockSpec(masks.shape, lambda b, c: (0, 0, 0)),
            pl.BlockSpec((1, GLA_DV), lambda b, c: (0, 0)),
        ],
        out_specs=pl.BlockSpec((1, C, GLA_V), lambda b, c: (b, c, 0)),
        out_shape=jax.ShapeDtypeStruct((B, S, GLA_V), BF16),
        scratch_shapes=[pltpu.VMEM((GLA_QK, GLA_DV), F32)],
        compiler_params=_params("parallel", "arbitrary"),
        name="gla",
    )(pg3, pg3, pg3, pg3, la3, mstack, masks, g_gla_out.reshape(1, GLA_DV))


SB_BLK = 256


def _sb_kernel(q_ref, k_ref, v_ref, u_ref, o_ref, acc_ref, run_ref):
    tq = SB_BLK
    qi = pl.program_id(2)
    q = q_ref[0]
    lane = lax.broadcasted_iota(jnp.int32, (tq, LANES), 1)
    qms = [jnp.where((lane >= hd * SB_DH) & (lane < (hd + 1) * SB_DH), q, jnp.zeros_like(q)) for hd in range(2)]
    acc_ref[...] = jnp.zeros_like(acc_ref)
    run_ref[...] = jnp.zeros_like(run_ref)

    def block(kb, masked):
        start = pl.multiple_of(kb * tq, tq)
        kblk = k_ref[0, pl.ds(start, tq), :]
        vblk = v_ref[0, pl.ds(start, tq), :]
        u = u_ref[...]
        if masked:
            causal = (lax.broadcasted_iota(jnp.int32, (tq, tq), 1) < lax.broadcasted_iota(jnp.int32, (tq, tq), 0))
        for hd in range(2):
            z = _nt(qms[hd], kblk)
            sp = jnp.log1p(jnp.exp(-jnp.abs(z)))
            ls = jnp.minimum(z, 0.0) - sp
            l1m = -jnp.maximum(z, 0.0) - sp
            if masked:
                l1m = jnp.where(causal, l1m, 0.0)
            hi, lo = _split(l1m)
            suf = _mm(hi, u) + _mm(lo, u)
            run = run_ref[hd]
            a = jnp.exp(ls + suf + run)
            if masked:
                a = jnp.where(causal, a, 0.0)
            acc_ref[hd] += _mm(a.astype(BF16), vblk)
            run_ref[hd] = run + suf[:, 0:1] + l1m[:, 0:1]

    block(qi, True)

    def body(i, carry):
        block(qi - 1 - i, False)
        return carry

    lax.fori_loop(0, qi, body, 0)
    o_ref[0] = jnp.where(lane < SB_DH, acc_ref[0], acc_ref[1]).astype(BF16)


def _stick_breaking(ps3):
    B, S, _ = ps3.shape
    tq = SB_BLK
    pairs = SB_HEADS // 2
    u = jnp.asarray(np.arange(tq)[:, None] > np.arange(tq)[None, :], BF16)
    return pl.pallas_call(
        _sb_kernel,
        grid=(B, pairs, S // tq),
        in_specs=[
            pl.BlockSpec((1, tq, LANES), lambda b, p, i: (b, i, p)),
            pl.BlockSpec((1, S, LANES), lambda b, p, i: (b, 0, pairs + p)),
            pl.BlockSpec((1, S, LANES), lambda b, p, i: (b, 0, 2 * pairs + p)),
            pl.BlockSpec((tq, tq), lambda b, p, i: (0, 0)),
        ],
        out_specs=pl.BlockSpec((1, tq, LANES), lambda b, p, i: (b, i, p)),
        out_shape=jax.ShapeDtypeStruct((B, S, SB_W), BF16),
        scratch_shapes=[pltpu.VMEM((2, tq, LANES), F32), pltpu.VMEM((2, tq, 1), F32)],
        compiler_params=_params("parallel", "parallel", "arbitrary"),
        name="stickbreak",
    )(ps3, ps3, ps3, u)


def _outproj_kernel(x_ref, og_ref, os_ref, wo_ref, g_ref, wq_ref, h_ref, q_ref):
    h = x_ref[...] + _mm(og_ref[...], wo_ref[:GLA_V]) + _mm(os_ref[...], wo_ref[GLA_V:])
    h_ref[...] = h
    q_ref[...] = (_mm(_rms(h, g_ref[...]).astype(BF16), wq_ref[...]) * (MEM_DH ** -0.5)).astype(BF16)


def _outproj(x2, o_gla, o_sb, w_out, g_mem_q, w_mq, tm):
    T = x2.shape[0]
    row = lambda w: pl.BlockSpec((tm, w), lambda i: (i, 0))
    full = lambda r, c: pl.BlockSpec((r, c), lambda i: (0, 0))
    return pl.pallas_call(
        _outproj_kernel,
        grid=(T // tm,),
        in_specs=[row(D_MODEL), row(GLA_V), row(SB_W), full(GLA_V + SB_W, D_MODEL),
                  full(1, D_MODEL), full(D_MODEL, D_MODEL)],
        out_specs=[row(D_MODEL), row(D_MODEL)],
        out_shape=[jax.ShapeDtypeStruct((T, D_MODEL), F32), jax.ShapeDtypeStruct((T, D_MODEL), BF16)],
        compiler_params=_params("parallel"),
        name="outproj",
    )(x2, o_gla, o_sb, w_out.astype(BF16), g_mem_q.reshape(1, D_MODEL), w_mq.astype(BF16))


def _memkv_kernel(m_ref, g_ref, w_ref, kv_ref):
    kv_ref[...] = _mm(_rms(m_ref[...], g_ref[...]).astype(BF16), w_ref[...]).astype(BF16)


def _memkv(mem2, g_mem_kv, w_mkv, tm):
    R = mem2.shape[0]
    return pl.pallas_call(
        _memkv_kernel,
        grid=(R // tm,),
        in_specs=[pl.BlockSpec((tm, D_MODEL), lambda i: (i, 0)),
                  pl.BlockSpec((1, D_MODEL), lambda i: (0, 0)),
                  pl.BlockSpec((D_MODEL, 2 * D_MODEL), lambda i: (0, 0))],
        out_specs=pl.BlockSpec((tm, 2 * D_MODEL), lambda i: (i, 0)),
        out_shape=jax.ShapeDtypeStruct((R, 2 * D_MODEL), BF16),
        compiler_params=_params("parallel"),
        name="memkv",
    )(mem2, g_mem_kv.reshape(1, D_MODEL), w_mkv.astype(BF16))


def _memattn_kernel(h_ref, q_ref, k_ref, v_ref, wo_ref, g_ref, wpq_ref, h2_ref, xn_ref, qp_ref):
    q = q_ref[0]
    outs = []
    for hd in range(MEM_HEADS):
        sl = slice(hd * MEM_DH, (hd + 1) * MEM_DH)
        s = _nt(q[:, sl], k_ref[0, :, sl])
        p = jnp.exp(s - jnp.max(s, axis=-1, keepdims=True))
        p = p / jnp.sum(p, axis=-1, keepdims=True)
        outs.append(_mm(p.astype(BF16), v_ref[0, :, sl]).astype(BF16))
    o = jnp.concatenate(outs, axis=-1)
    h2 = h_ref[0] + _mm(o, wo_ref[...])
    h2_ref[0] = h2
    xn = _rms(h2, g_ref[...]).astype(BF16)
    xn_ref[0] = xn
    qp = _mm(xn, wpq_ref[...]).astype(BF16)
    for hp in range(2 * PEER_HEADS):
        qp_ref[hp, 0] = qp[:, hp * LANES:(hp + 1) * LANES]


def _memattn(h1_3, qm_3, kv_3, w_mo, g_ffn, w_pq, tm):
    B, S, _ = h1_3.shape
    M = kv_3.shape[1]
    nq = PEER_HEADS * PEER_DQ
    row = lambda w: pl.BlockSpec((1, tm, w), lambda b, i: (b, i, 0))
    full = lambda r, c: pl.BlockSpec((r, c), lambda b, i: (0, 0))
    return pl.pallas_call(
        _memattn_kernel,
        grid=(B, S // tm),
        in_specs=[row(D_MODEL), row(D_MODEL),
                  pl.BlockSpec((1, M, D_MODEL), lambda b, i: (b, 0, 0)),
                  pl.BlockSpec((1, M, D_MODEL), lambda b, i: (b, 0, 1)),
                  full(D_MODEL, D_MODEL), full(1, D_MODEL), full(D_MODEL, nq)],
        out_specs=[row(D_MODEL), row(D_MODEL),
                   pl.BlockSpec((2 * PEER_HEADS, 1, tm, LANES), lambda b, i: (0, b, i, 0))],
        out_shape=[jax.ShapeDtypeStruct((B, S, D_MODEL), F32),
                   jax.ShapeDtypeStruct((B, S, D_MODEL), BF16),
                   jax.ShapeDtypeStruct((2 * PEER_HEADS, B, S, LANES), BF16)],
        compiler_params=_params("parallel", "parallel"),
        name="memattn",
    )(h1_3, qm_3, kv_3, kv_3, w_mo.astype(BF16), g_ffn.reshape(1, D_MODEL), w_pq.astype(BF16))


def _extract_topk(vals, ids, out_val_ref, out_idx_ref, base):
    big = 2.0 * PEER_NKEYS * PEER_NKEYS

    def body(kk, s):
        m = jnp.max(s, axis=0, keepdims=True)
        idx = jnp.min(jnp.where(s == m, ids, big), axis=0, keepdims=True)
        out_val_ref[pl.ds(base + kk, 1), :] = m
        out_idx_ref[pl.ds(base + kk, 1), :] = idx
        return jnp.where(ids == idx, -jnp.inf, s)

    lax.fori_loop(0, PEER_TOPK, body, vals)


def _pair_candidates():
    K = PEER_TOPK
    groups = [("row0", 0, 0, K), ("row0", 1, 0, K // 2), ("col0", 0, K // 2, K)]
    ids = [0 * K + b for b in range(K)] + [1 * K + b for b in range(K // 2)] + [a * K for a in range(K // 2, K)]
    for b in range(K):
        valid = [a for a in range(2, K // 2) if (a + 1) * (b + 1) <= K]
        if not valid:
            break
        groups.append(("col0", b, 0, K // 2))
        ids += [(a * K + b if a in valid else -1) for a in range(K // 2)]
    covered = sorted(i for i in ids if i >= 0)
    want = sorted(a * K + b for a in range(K) for b in range(K) if (a + 1) * (b + 1) <= K)
    assert covered == want, (covered, want)
    return groups, np.asarray(ids, np.float32)


def _select_rows(idx, table):
    out = jnp.zeros(idx.shape, table.dtype)
    for a in range(PEER_TOPK):
        out = jnp.where(idx == a, table[a:a + 1, :], out)
    return out


def _route_kernel(qp_ref, keys_ref, ids_ref, i1_ref, i2_ref, gate_ref, ts_ref, ti_ref, bs_ref, bj_ref):
    tm = qp_ref.shape[1]
    K = PEER_TOPK
    groups, _ = _pair_candidates()

    def stage1(hp, carry):
        s = _nt(keys_ref[hp], qp_ref[hp])
        key_ids = lax.broadcasted_iota(jnp.int32, (PEER_NKEYS, tm), 0).astype(F32)
        _extract_topk(s, key_ids, ts_ref, ti_ref, hp * K)
        return carry

    lax.fori_loop(0, 2 * PEER_HEADS, stage1, 0)

    def stage2(h, carry):
        r0 = pl.multiple_of(2 * h * K, K)
        r1 = pl.multiple_of((2 * h + 1) * K, K)
        s0 = ts_ref[pl.ds(r0, K), :]
        s1 = ts_ref[pl.ds(r1, K), :]
        parts = []
        for kind, fixed, lo, hi in groups:
            if kind == "row0":
                parts.append(s0[fixed:fixed + 1, :] + s1[lo:hi, :])
            else:
                parts.append(s0[lo:hi, :] + s1[fixed:fixed + 1, :])
        ids = ids_ref[...]
        cand = jnp.where(ids >= 0.0, jnp.concatenate(parts, axis=0), -jnp.inf)
        _extract_topk(cand, jnp.where(ids >= 0.0, ids, 2.0 * PEER_N), bs_ref, bj_ref, 0)
        best = bs_ref[...]
        j = bj_ref[...].astype(jnp.int32)
        ex = jnp.exp(best - best[0:1, :])
        out = pl.multiple_of(h * K, K)
        gate_ref[pl.ds(out, K), :] = ex / jnp.sum(ex, axis=0, keepdims=True)
        i1_ref[pl.ds(out, K), :] = _select_rows(j >> 4, ti_ref[pl.ds(r0, K), :])
        i2_ref[pl.ds(out, K), :] = _select_rows(j & (K - 1), ti_ref[pl.ds(r1, K), :])
        return carry

    lax.fori_loop(0, PEER_HEADS, stage2, 0)


def _route(qp3, sub_keys, tm):
    T = qp3.shape[1]
    keys = sub_keys.reshape(2 * PEER_HEADS, PEER_NKEYS, PEER_DQ // 2).astype(BF16)
    out = lambda: pl.BlockSpec((PEER_HK, tm), lambda i: (0, i))
    stage1_rows = 2 * PEER_HEADS * PEER_TOPK
    pair_ids = jnp.asarray(np.broadcast_to(_pair_candidates()[1][:, None], (_pair_candidates()[1].shape[0], tm)))
    return pl.pallas_call(
        _route_kernel,
        grid=(T // tm,),
        in_specs=[pl.BlockSpec((2 * PEER_HEADS, tm, LANES), lambda i: (0, i, 0)),
                  pl.BlockSpec(keys.shape, lambda i: (0, 0, 0)),
                  pl.BlockSpec(pair_ids.shape, lambda i: (0, 0))],
        out_specs=[out(), out(), out()],
        out_shape=[jax.ShapeDtypeStruct((PEER_HK, T), F32)] * 3,
        scratch_shapes=[pltpu.VMEM((stage1_rows, tm), F32),
                        pltpu.VMEM((stage1_rows, tm), F32),
                        pltpu.VMEM((PEER_TOPK, tm), F32),
                        pltpu.VMEM((PEER_TOPK, tm), F32)],
        compiler_params=_params("parallel"),
        name="peer_route",
    )(qp3, keys, pair_ids)


GATE_TOKENS = 128
GATE_PITCH = GATE_TOKENS + 8


def _gates_kernel(i1_ref, i2_ref, gate_ref, g_ref, a_ref, b_ref, w_ref, gs_ref):
    tg = GATE_TOKENS
    a_ref[...] = jnp.transpose(i1_ref[...])
    b_ref[...] = jnp.transpose(i2_ref[...])
    w_ref[...] = jnp.transpose(gate_ref[...])
    sub = lax.broadcasted_iota(jnp.int32, (PEER_NKEYS, PEER_HK), 0).astype(F32)

    def token(t, carry):
        a = a_ref[pl.ds(t, 1), :]
        b = b_ref[pl.ds(t, 1), :]
        w = w_ref[pl.ds(t, 1), :]
        oh1 = jnp.where(sub == a, 1.0, 0.0).astype(BF16)
        oh2 = jnp.where(sub == b, w, 0.0).astype(BF16)
        gs_ref[pl.ds(t, PEER_NKEYS, stride=GATE_PITCH), :] = _nt(oh1, oh2)
        return carry

    lax.fori_loop(0, tg, token, 0, unroll=8)

    def slab(s, carry):
        g_ref[s] = gs_ref[pl.ds(pl.multiple_of(s * GATE_PITCH, 8), tg), :].astype(BF16)
        return carry

    lax.fori_loop(0, PEER_NKEYS, slab, 0, unroll=4)


def _gates(i1, i2, gate):
    T = i1.shape[1]
    tg = GATE_TOKENS
    blk = lambda: pl.BlockSpec((PEER_HK, tg), lambda i: (0, i))
    return pl.pallas_call(
        _gates_kernel,
        grid=(T // tg,),
        in_specs=[blk(), blk(), blk()],
        out_specs=pl.BlockSpec((PEER_NKEYS, tg, PEER_NKEYS), lambda i: (0, i, 0)),
        out_shape=jax.ShapeDtypeStruct((PEER_NKEYS, T, PEER_NKEYS), BF16),
        scratch_shapes=[pltpu.VMEM((tg, PEER_HK), F32)] * 3 + [pltpu.VMEM((PEER_NKEYS * GATE_PITCH, PEER_NKEYS), F32)],
        compiler_params=_params("parallel"),
        name="peer_gates",
    )(i1, i2, gate)


def _peer_kernel(xn_ref, ut_ref, g_ref, v_ref, h_ref, gf_ref, o_ref, acc_ref):
    j = pl.program_id(1)

    @pl.when(j == 0)
    def _():
        acc_ref[...] = jnp.zeros_like(acc_ref)

    p = _mm(xn_ref[...], ut_ref[...])
    act = 0.5 * p * (1.0 + lax.erf(p * (2.0 ** -0.5)))
    w = jnp.concatenate(
        [(g_ref[s].astype(F32) * act[:, s * LANES:(s + 1) * LANES]).astype(BF16) for s in range(g_ref.shape[0])],
        axis=-1)
    acc_ref[...] += _mm(w, v_ref[...])

    @pl.when(j == pl.num_programs(1) - 1)
    def _():
        o_ref[...] = _rms(h_ref[...] + acc_ref[...], gf_ref[...])


def _peer(xn2, g3, peer_u, peer_v, h2, g_final, tb, nb):
    T = xn2.shape[0]
    ut = peer_u.T.astype(BF16)
    vb = peer_v.astype(BF16)
    return pl.pallas_call(
        _peer_kernel,
        grid=(T // tb, PEER_N // nb),
        in_specs=[pl.BlockSpec((tb, D_MODEL), lambda i, j: (i, 0)),
                  pl.BlockSpec((D_MODEL, nb), lambda i, j: (0, j)),
                  pl.BlockSpec((nb // LANES, tb, LANES), lambda i, j: (j, i, 0)),
                  pl.BlockSpec((nb, D_MODEL), lambda i, j: (j, 0)),
                  pl.BlockSpec((tb, D_MODEL), lambda i, j: (i, 0)),
                  pl.BlockSpec((1, D_MODEL), lambda i, j: (0, 0))],
        out_specs=pl.BlockSpec((tb, D_MODEL), lambda i, j: (i, 0)),
        out_shape=jax.ShapeDtypeStruct((T, D_MODEL), F32),
        scratch_shapes=[pltpu.VMEM((tb, D_MODEL), F32)],
        compiler_params=_params("parallel", "arbitrary"),
        name="peer_dense",
    )(xn2, ut, g3, vb, h2, g_final.reshape(1, D_MODEL))


def _tile(n, want):
    t = min(n, want)
    assert n % t == 0, (n, t)
    return t


def kernel(x, mem, g_mix, w_in, w_alpha_up, b_alpha, g_gla_out, w_out, g_mem_q, g_mem_kv, w_mq, w_mkv, w_mo,
           g_ffn, w_pq, sub_keys, peer_u, peer_v, g_final):
    B, S, D = x.shape
    T = B * S
    assert D == D_MODEL and S % SB_BLK == 0 and g_mix.shape[0] == 1
    x2 = x.reshape(T, D)

    pg, la, ps = _inproj(x2, g_mix[0], w_in[0], w_alpha_up[0], b_alpha[0], _tile(T, 256))
    o_gla = _gla(pg.reshape(B, S, -1), la.reshape(B, S, -1), g_gla_out[0])
    o_sb = _stick_breaking(ps.reshape(B, S, -1))
    h1, qm = _outproj(x2, o_gla.reshape(T, -1), o_sb.reshape(T, -1), w_out[0], g_mem_q[0], w_mq[0], _tile(T, 512))
    M = mem.shape[1]
    kv = _memkv(mem.reshape(B * M, D), g_mem_kv[0], w_mkv[0], _tile(B * M, 256))
    h2, xn, qp = _memattn(h1.reshape(B, S, D), qm.reshape(B, S, D), kv.reshape(B, M, 2 * D),
                          w_mo[0], g_ffn[0], w_pq[0], _tile(S, 256))
    i1, i2, gate = _route(qp.reshape(2 * PEER_HEADS, T, LANES), sub_keys[0], _tile(T, 256))
    gmat = _gates(i1, i2, gate)
    out = _peer(xn.reshape(T, D), gmat, peer_u[0], peer_v[0], h2.reshape(T, D), g_final, _tile(T, 1024), 512)
    return out.reshape(B, S, D)
```

```python
import functools

import numpy as np
import jax
import jax.numpy as jnp
from jax import lax
from jax.experimental import pallas as pl
from jax.experimental.pallas import tpu as pltpu

F32 = jnp.float32
BF16 = jnp.bfloat16
EPS = 1e-6
LOG2E = 1.4426950408889634

D_MODEL = 1024
CHUNK = 64
GLA_HEADS, GLA_DK, GLA_DV, GLA_RANK, GLA_TAU = 4, 64, 128, 16, 16.0
GLA_QK = GLA_HEADS * GLA_DK
GLA_V = GLA_HEADS * GLA_DV
SB_HEADS, SB_DH = 8, 64
SB_W = SB_HEADS * SB_DH
MEM_HEADS = 4
MEM_DH = D_MODEL // MEM_HEADS
PEER_HEADS, PEER_NKEYS, PEER_DQ, PEER_TOPK = 8, 128, 256, 16
PEER_N = PEER_NKEYS * PEER_NKEYS
PEER_HK = PEER_HEADS * PEER_TOPK
LANES = 128
SUBLANES = 8
GLA_LEVELS = 6
V7X_VMEM_BYTES = 64 * 1024 * 1024
VMEM_LIMIT = V7X_VMEM_BYTES - 8 * 1024 * 1024

TILE_INPROJ = 1024
TILE_OUTPROJ = 1024
TILE_MEMKV = 256
TILE_MEMATTN = 1024
TILE_PEER = (2048, 512)


def _nt(a, b):
    return lax.dot_general(a, b, (((1,), (1,)), ((), ())), preferred_element_type=F32)


def _mm(a, b):
    return jnp.dot(a, b, preferred_element_type=F32)


def _split(x):
    hi = x.astype(BF16)
    lo = (x - hi.astype(F32)).astype(BF16)
    return hi, lo


def _rms(x, g):
    return x * lax.rsqrt(jnp.mean(x * x, axis=-1, keepdims=True) + EPS) * g


def _log_sigmoid(x):
    return jnp.minimum(x, 0.0) - jnp.log1p(jnp.exp(-jnp.abs(x)))


def _params(*sem):
    return pltpu.CompilerParams(dimension_semantics=sem, vmem_limit_bytes=VMEM_LIMIT)


def _mm3(a, b):
    a_hi, a_lo = _split(a)
    b_hi, b_lo = _split(b)
    return _mm(a_hi, b_hi) + _mm(a_lo, b_hi) + _mm(a_hi, b_lo)


def _inproj_kernel(x_ref, g_ref, w_ref, wa_ref, wup_ref, ba_ref, pg_ref, la_ref, ps_ref):
    hn32 = _rms(x_ref[...], g_ref[...])
    y = _mm(hn32.astype(BF16), w_ref[...])
    ng = GLA_QK * 2 + GLA_V * 2
    pg_ref[:, :GLA_QK] = y[:, :GLA_QK] * (GLA_DK ** -0.5)
    pg_ref[:, GLA_QK:] = y[:, GLA_QK:ng]
    ps_ref[:, :SB_W] = (y[:, ng:ng + SB_W] * (SB_DH ** -0.5 * LOG2E)).astype(BF16)
    ps_ref[:, SB_W:] = y[:, ng + SB_W:].astype(BF16)
    alr = _mm3(hn32, wa_ref[...])
    pre = _mm3(alr, wup_ref[...]) + ba_ref[...]
    la_ref[...] = _log_sigmoid(pre) * (1.0 / GLA_TAU)


def _inproj(x2, g_mix, w_in, w_alpha_up, b_alpha, tm):
    T = x2.shape[0]
    ng = GLA_QK * 2 + GLA_V * 2
    w_cat = jnp.concatenate([w_in[:, :ng], w_in[:, ng + GLA_RANK:]], axis=1).astype(BF16)
    w_alr = jnp.pad(w_in[:, ng:ng + GLA_RANK], ((0, 0), (0, LANES - GLA_RANK)))
    wup = jnp.pad(w_alpha_up, ((0, LANES - GLA_RANK), (0, 0)))
    nw = w_cat.shape[1]
    return pl.pallas_call(
        _inproj_kernel,
        grid=(T // tm,),
        in_specs=[
            pl.BlockSpec((tm, D_MODEL), lambda i: (i, 0)),
            pl.BlockSpec((1, D_MODEL), lambda i: (0, 0)),
            pl.BlockSpec((D_MODEL, nw), lambda i: (0, 0)),
            pl.BlockSpec((D_MODEL, LANES), lambda i: (0, 0)),
            pl.BlockSpec((LANES, GLA_QK), lambda i: (0, 0)),
            pl.BlockSpec((1, GLA_QK), lambda i: (0, 0)),
        ],
        out_specs=[
            pl.BlockSpec((tm, ng), lambda i: (i, 0)),
            pl.BlockSpec((tm, GLA_QK), lambda i: (i, 0)),
            pl.BlockSpec((tm, 3 * SB_W), lambda i: (i, 0)),
        ],
        out_shape=[
            jax.ShapeDtypeStruct((T, ng), F32),
            jax.ShapeDtypeStruct((T, GLA_QK), F32),
            jax.ShapeDtypeStruct((T, 3 * SB_W), BF16),
        ],
        compiler_params=_params("parallel"),
        name="inproj",
    )(x2, g_mix.reshape(1, D_MODEL), w_cat, w_alr, wup, b_alpha.reshape(1, GLA_QK))


def _gla_constants():
    C = CHUNK
    t = np.arange(C)[:, None]
    j = np.arange(C)[None, :]
    mats = [(j <= t), (j > t)]
    masks = []
    for l in range(GLA_LEVELS):
        low = (1 << l) - 1
        mats.append((j > (t & ~low)) & (j <= t))
        mats.append((j > t) & (j <= np.minimum((t | low) + 1, C - 1)))
        s = j
        masks.append(((t >> (l + 1)) == (s >> (l + 1))) & (((t >> l) & 1) == 1) & (((s >> l) & 1) == 0))
    masks.append(t == j)
    m = np.concatenate(mats, axis=0).astype(np.float32)
    m = np.concatenate([m, m], axis=1)
    masks = np.tile(np.stack(masks).astype(np.float32), (1, GLA_HEADS, 1))
    return jnp.asarray(m, BF16), jnp.asarray(masks)


def _gla_kernel(q_ref, k_ref, v_ref, og_ref, la_ref, m_ref, mask_ref, gout_ref, o_ref, state_ref):
    C = CHUNK

    @pl.when(pl.program_id(1) == 0)
    def _():
        state_ref[...] = jnp.zeros_like(state_ref)

    lane = lax.broadcasted_iota(jnp.int32, (C, GLA_QK), 1)
    hms = [(lane >= h * GLA_DK) & (lane < (h + 1) * GLA_DK) for h in range(GLA_HEADS)]

    def stack_heads(x):
        return jnp.concatenate([jnp.where(hm, x, 0.0) for hm in hms], axis=0).astype(BF16)

    m = m_ref[...]
    gout = gout_ref[...]
    seqs = range(q_ref.shape[0])
    q = [q_ref[s] for s in seqs]
    k = [k_ref[s] for s in seqs]
    e = [jnp.exp(_mm(m, jnp.concatenate(_split(la_ref[s]), axis=0))) for s in seqs]
    qd = [q[s] * e[s][0:C] for s in seqs]
    kd = [k[s] * e[s][C:2 * C] for s in seqs]
    state = [state_ref[s] for s in seqs]
    inter_all = [_mm(stack_heads(qd[s]), state[s].astype(BF16)) for s in seqs]
    attn_all = [jnp.zeros((GLA_HEADS * C, C), F32) for s in seqs]
    for l in range(GLA_LEVELS + 1):
        for s in seqs:
            if l < GLA_LEVELS:
                q_l = q[s] * e[s][(2 + 2 * l) * C:(3 + 2 * l) * C]
                k_l = k[s] * e[s][(3 + 2 * l) * C:(4 + 2 * l) * C]
            else:
                q_l, k_l = q[s], k[s]
            attn_all[s] = attn_all[s] + mask_ref[l] * _nt(stack_heads(q_l), k_l.astype(BF16))
    kdt = [jnp.transpose(kd[s]).astype(BF16) for s in seqs]
    new_rows = [[] for s in seqs]
    for h in range(GLA_HEADS):
        for s in seqs:
            v_h = v_ref[s, :, h * GLA_DV:(h + 1) * GLA_DV].astype(BF16)
            o = inter_all[s][h * C:(h + 1) * C] + _mm(attn_all[s][h * C:(h + 1) * C].astype(BF16), v_h)
            o = _rms(o, gout)
            og = og_ref[s, :, h * GLA_DV:(h + 1) * GLA_DV]
            o_ref[s, :, h * GLA_DV:(h + 1) * GLA_DV] = (o * (og * jax.nn.sigmoid(og))).astype(BF16)
            new_rows[s].append(_mm(kdt[s][h * GLA_DK:(h + 1) * GLA_DK], v_h))
    for s in seqs:
        dec = jnp.transpose(jnp.broadcast_to(e[s][C - 1:C], (LANES, GLA_QK)))
        state_ref[s] = state[s] * dec + jnp.concatenate(new_rows[s], axis=0)


def _gla(pg3, la3, g_gla_out):
    B, S, _ = pg3.shape
    mstack, masks = _gla_constants()
    C = CHUNK
    nseq = next(n for n in (8, 4, 2, 1) if B % n == 0)
    return pl.pallas_call(
        _gla_kernel,
        grid=(B // nseq, S // C),
        in_specs=[
            pl.BlockSpec((nseq, C, GLA_QK), lambda b, c: (b, c, 0)),
            pl.BlockSpec((nseq, C, GLA_QK), lambda b, c: (b, c, 1)),
            pl.BlockSpec((nseq, C, GLA_V), lambda b, c: (b, c, 1)),
            pl.BlockSpec((nseq, C, GLA_V), lambda b, c: (b, c, 2)),
            pl.BlockSpec((nseq, C, GLA_QK), lambda b, c: (b, c, 0)),
            pl.BlockSpec(mstack.shape, lambda b, c: (0, 0)),
            pl.BlockSpec(masks.shape, lambda b, c: (0, 0, 0)),
            pl.BlockSpec((1, GLA_DV), lambda b, c: (0, 0)),
        ],
        out_specs=pl.BlockSpec((nseq, C, GLA_V), lambda b, c: (b, c, 0)),
        out_shape=jax.ShapeDtypeStruct((B, S, GLA_V), BF16),
        scratch_shapes=[pltpu.VMEM((nseq, GLA_QK, GLA_DV), F32)],
        compiler_params=_params("parallel", "arbitrary"),
        name="gla",
    )(pg3, pg3, pg3, pg3, la3, mstack, masks, g_gla_out.reshape(1, GLA_DV))


SB_BLK = 256


def _sb_steps(nq):
    steps = [(i, j, int(j == i)) for i in range(nq) for j in range(i, -1, -1)]
    return np.asarray(steps, np.int32).T


def _sb_kernel(tab_ref, q_ref, k_ref, v_ref, u_ref, bx_ref, o_ref, z_ref, x_ref, acc_ref, run_ref, *, nsteps):
    tq = SB_BLK
    lane = lax.broadcasted_iota(jnp.int32, (tq, LANES), 1)

    def rows(i):
        return pl.ds(pl.multiple_of(i * tq, tq), tq)

    def scores(n, slot):
        q = q_ref[0, rows(tab_ref[0, n]), :]
        qs = jnp.concatenate([jnp.where(lane < SB_DH, q, jnp.zeros_like(q)),
                              jnp.where(lane >= SB_DH, q, jnp.zeros_like(q))], axis=0)
        z_ref[slot] = _nt(qs, k_ref[0, rows(tab_ref[1, n]), :])

    def logits(n, slot):
        first = tab_ref[2, n]
        z = z_ref[slot] + bx_ref[first]
        pos = jnp.maximum(z, 0.0)
        neg = z - pos
        sp = jnp.log2(1.0 + jnp.exp2(neg - pos))
        ls = neg - sp
        nl1m = pos + sp
        suf = _mm(nl1m.astype(BF16), u_ref[...])
        run = run_ref[...] * (1 - first).astype(F32)
        x_ref[slot] = ls - suf - run
        run_ref[...] = run + suf[:, 0:1] + nl1m[:, 0:1]

    def values(n, slot):
        a = jnp.exp2(x_ref[slot]).astype(BF16)
        pv = _mm(a, v_ref[0, rows(tab_ref[1, n]), :])
        acc = acc_ref[...] * (1 - tab_ref[2, n]).astype(F32) + pv
        acc_ref[...] = acc
        o_ref[0, rows(tab_ref[0, n]), :] = jnp.where(lane < SB_DH, acc[:tq], acc[tq:]).astype(BF16)

    acc_ref[...] = jnp.zeros_like(acc_ref)
    run_ref[...] = jnp.zeros_like(run_ref)
    base = nsteps % 2
    if base:
        scores(0, 0)
        logits(0, 0)
        values(0, 0)
    m = nsteps - base
    if m:
        scores(base, 0)
        scores(base + 1, 1)
        logits(base, 0)

        def body(j, carry):
            t = base + 2 * j
            scores(t, 0)
            logits(t - 1, 1)
            values(t - 2, 0)
            scores(t + 1, 1)
            logits(t, 0)
            values(t - 1, 1)
            return carry

        lax.fori_loop(1, m // 2, body, 0)
        logits(nsteps - 1, 1)
        values(nsteps - 2, 0)
        values(nsteps - 1, 1)


def _stick_breaking(ps3):
    B, S, _ = ps3.shape
    tq = SB_BLK
    pairs = SB_HEADS // 2
    tab = jnp.asarray(_sb_steps(S // tq))
    nsteps = tab.shape[1]
    r = np.arange(tq)
    u = jnp.asarray(r[:, None] > r[None, :], BF16)
    causal = np.stack([np.ones((tq, tq), bool), r[None, :] < r[:, None]])
    causal = np.concatenate([causal, causal], axis=1)
    bx = jnp.asarray(np.where(causal, 0.0, -np.inf), F32)
    seq = lambda c: pl.BlockSpec((1, S, LANES), lambda b, p, tab: (b, 0, c * pairs + p))
    const = lambda a: pl.BlockSpec(a.shape, lambda b, p, tab: (0,) * a.ndim)
    return pl.pallas_call(
        functools.partial(_sb_kernel, nsteps=nsteps),
        grid_spec=pltpu.PrefetchScalarGridSpec(
            num_scalar_prefetch=1,
            grid=(B, pairs),
            in_specs=[seq(0), seq(1), seq(2), const(u), const(bx)],
            out_specs=pl.BlockSpec((1, S, LANES), lambda b, p, tab: (b, 0, p)),
            scratch_shapes=[pltpu.VMEM((2, 2 * tq, tq), F32), pltpu.VMEM((2, 2 * tq, tq), F32),
                            pltpu.VMEM((2 * tq, LANES), F32), pltpu.VMEM((2 * tq, 1), F32)],
        ),
        out_shape=jax.ShapeDtypeStruct((B, S, SB_W), BF16),
        compiler_params=_params("parallel", "parallel"),
        name="stickbreak",
    )(tab, ps3, ps3, ps3, u, bx)


def _outproj_kernel(x_ref, og_ref, os_ref, wo_ref, g_ref, wq_ref, h_ref, q_ref):
    h = x_ref[...] + _mm(og_ref[...], wo_ref[:GLA_V]) + _mm(os_ref[...], wo_ref[GLA_V:])
    h_ref[...] = h
    q_ref[...] = (_mm(_rms(h, g_ref[...]).astype(BF16), wq_ref[...]) * (MEM_DH ** -0.5)).astype(BF16)


def _outproj(x2, o_gla, o_sb, w_out, g_mem_q, w_mq, tm):
    T = x2.shape[0]
    row = lambda w: pl.BlockSpec((tm, w), lambda i: (i, 0))
    full = lambda r, c: pl.BlockSpec((r, c), lambda i: (0, 0))
    return pl.pallas_call(
        _outproj_kernel,
        grid=(T // tm,),
        in_specs=[row(D_MODEL), row(GLA_V), row(SB_W), full(GLA_V + SB_W, D_MODEL),
                  full(1, D_MODEL), full(D_MODEL, D_MODEL)],
        out_specs=[row(D_MODEL), row(D_MODEL)],
        out_shape=[jax.ShapeDtypeStruct((T, D_MODEL), F32), jax.ShapeDtypeStruct((T, D_MODEL), BF16)],
        compiler_params=_params("parallel"),
        name="outproj",
    )(x2, o_gla, o_sb, w_out.astype(BF16), g_mem_q.reshape(1, D_MODEL), w_mq.astype(BF16))


def _memkv_kernel(m_ref, g_ref, w_ref, kv_ref):
    kv_ref[...] = _mm(_rms(m_ref[...], g_ref[...]).astype(BF16), w_ref[...]).astype(BF16)


def _memkv(mem2, g_mem_kv, w_mkv, tm):
    R = mem2.shape[0]
    return pl.pallas_call(
        _memkv_kernel,
        grid=(R // tm,),
        in_specs=[pl.BlockSpec((tm, D_MODEL), lambda i: (i, 0)),
                  pl.BlockSpec((1, D_MODEL), lambda i: (0, 0)),
                  pl.BlockSpec((D_MODEL, 2 * D_MODEL), lambda i: (0, 0))],
        out_specs=pl.BlockSpec((tm, 2 * D_MODEL), lambda i: (i, 0)),
        out_shape=jax.ShapeDtypeStruct((R, 2 * D_MODEL), BF16),
        compiler_params=_params("parallel"),
        name="memkv",
    )(mem2, g_mem_kv.reshape(1, D_MODEL), w_mkv.astype(BF16))


def _memattn_kernel(h_ref, q_ref, k_ref, v_ref, wo_ref, g_ref, wpq_ref, h2_ref, xn_ref, qp_ref):
    q = q_ref[0]
    heads = [slice(hd * MEM_DH, (hd + 1) * MEM_DH) for hd in range(MEM_HEADS)]
    s = [_nt(q[:, sl], k_ref[0, :, sl]) for sl in heads]
    p = [jnp.exp(x - jnp.max(x, axis=-1, keepdims=True)) for x in s]
    p = [x / jnp.sum(x, axis=-1, keepdims=True) for x in p]
    o = jnp.concatenate([_mm(x.astype(BF16), v_ref[0, :, sl]).astype(BF16) for x, sl in zip(p, heads)], axis=-1)
    h2 = h_ref[0] + _mm(o, wo_ref[...])
    h2_ref[0] = h2
    xn = _rms(h2, g_ref[...]).astype(BF16)
    xn_ref[0] = xn
    qp = _mm(xn, wpq_ref[...]).astype(BF16)
    for hp in range(2 * PEER_HEADS):
        qp_ref[hp, 0] = qp[:, hp * LANES:(hp + 1) * LANES]


def _memattn(h1_3, qm_3, kv_3, w_mo, g_ffn, w_pq, tm):
    B, S, _ = h1_3.shape
    M = kv_3.shape[1]
    nq = PEER_HEADS * PEER_DQ
    row = lambda w: pl.BlockSpec((1, tm, w), lambda b, i: (b, i, 0))
    full = lambda r, c: pl.BlockSpec((r, c), lambda b, i: (0, 0))
    return pl.pallas_call(
        _memattn_kernel,
        grid=(B, S // tm),
        in_specs=[row(D_MODEL), row(D_MODEL),
                  pl.BlockSpec((1, M, D_MODEL), lambda b, i: (b, 0, 0)),
                  pl.BlockSpec((1, M, D_MODEL), lambda b, i: (b, 0, 1)),
                  full(D_MODEL, D_MODEL), full(1, D_MODEL), full(D_MODEL, nq)],
        out_specs=[row(D_MODEL), row(D_MODEL),
                   pl.BlockSpec((2 * PEER_HEADS, 1, tm, LANES), lambda b, i: (0, b, i, 0))],
        out_shape=[jax.ShapeDtypeStruct((B, S, D_MODEL), F32),
                   jax.ShapeDtypeStruct((B, S, D_MODEL), BF16),
                   jax.ShapeDtypeStruct((2 * PEER_HEADS, B, S, LANES), BF16)],
        compiler_params=_params("parallel", "parallel"),
        name="memattn",
    )(h1_3, qm_3, kv_3, kv_3, w_mo.astype(BF16), g_ffn.reshape(1, D_MODEL), w_pq.astype(BF16))


def _extract_topk(vals, ids, out_val_ref, out_idx_ref, bases, side=None):
    big = 2.0 * PEER_NKEYS * PEER_NKEYS

    def body(kk, ss):
        if side is not None:
            side(kk)
        out = []
        for s, base in zip(ss, bases):
            m = jnp.max(s, axis=0, keepdims=True)
            idx = jnp.min(jnp.where(s == m, ids, big), axis=0, keepdims=True)
            out_val_ref[pl.ds(base + kk, 1), :] = m
            out_idx_ref[pl.ds(base + kk, 1), :] = idx
            out.append(jnp.where(ids == idx, -jnp.inf, s))
        return tuple(out)

    lax.fori_loop(0, PEER_TOPK, body, tuple(vals), unroll=4)


def _extract_topk_grouped(score_refs, out_val_ref, out_idx_ref, bases, side=None):
    rows = SUBLANES
    groups = PEER_NKEYS // rows
    tm = out_val_ref.shape[1]
    group_id = lax.broadcasted_iota(jnp.int32, (groups, tm), 0).astype(F32)
    row_id = lax.broadcasted_iota(jnp.int32, (rows, tm), 0).astype(F32)
    big = 2.0 * PEER_NKEYS
    neg_inf = -jnp.inf

    def group(ref, j):
        return ref[j * rows:(j + 1) * rows, :]

    def init(ref):
        gm = jnp.concatenate([jnp.max(group(ref, j), axis=0, keepdims=True) for j in range(groups)], axis=0)
        return gm, jnp.full((1, tm), jnp.inf, F32), jnp.full((1, tm), -1.0, F32)

    def body(kk, carry):
        if side is not None:
            side(kk)
        out = []
        for (gm, m_prev, i_prev), ref, base in zip(carry, score_refs, bases):
            m = jnp.max(gm, axis=0, keepdims=True)
            jstar = jnp.min(jnp.where(gm == m, group_id, big), axis=0, keepdims=True)
            jint = jstar.astype(jnp.int32)
            level = [group(ref, j) for j in range(groups)]
            bit = 1
            while len(level) > 1:
                take_odd = (jint & bit) != 0
                level = [jnp.where(take_odd, level[2 * i + 1], level[2 * i]) for i in range(len(level) // 2)]
                bit *= 2
            x = level[0]
            ids = jstar * float(rows) + row_id
            eq_id = jnp.where(x == m, ids, -1.0)
            floor_id = jnp.where(m == m_prev, i_prev, -1.0)
            idx = jnp.min(jnp.where(eq_id > floor_id, eq_id, big), axis=0, keepdims=True)
            rest = jnp.maximum(jnp.where(x < m, x, neg_inf), jnp.where(eq_id > idx, m, neg_inf))
            gm = jnp.where(group_id == jstar, jnp.max(rest, axis=0, keepdims=True), gm)
            out_val_ref[pl.ds(base + kk, 1), :] = m
            out_idx_ref[pl.ds(base + kk, 1), :] = idx
            out.append((gm, m, idx))
        return tuple(out)

    lax.fori_loop(0, PEER_TOPK, body, tuple(init(ref) for ref in score_refs), unroll=16)


def _pair_candidates():
    K = PEER_TOPK
    groups = [("row0", 0, 0, K), ("row0", 1, 0, K // 2), ("col0", 0, K // 2, K)]
    ids = [0 * K + b for b in range(K)] + [1 * K + b for b in range(K // 2)] + [a * K for a in range(K // 2, K)]
    singles = []
    for b in range(K):
        valid = [a for a in range(2, K // 2) if (a + 1) * (b + 1) <= K]
        if len(valid) > K // 4:
            groups.append(("col0", b, 0, K // 2))
            ids += [(a * K + b if a in valid else -1) for a in range(K // 2)]
        else:
            singles += [(a, b) for a in valid]
    for i in range(0, len(singles), 8):
        chunk = singles[i:i + 8]
        groups.append(("pairs", chunk + [chunk[-1]] * (8 - len(chunk)), 0, 0))
        ids += [a * K + b for a, b in chunk] + [-1] * (8 - len(chunk))
    covered = sorted(i for i in ids if i >= 0)
    want = sorted(a * K + b for a in range(K) for b in range(K) if (a + 1) * (b + 1) <= K)
    assert covered == want, (covered, want)
    return groups, np.asarray(ids, np.float32)


def _select_rows(idx, table):
    out = jnp.zeros(idx.shape, table.dtype)
    for a in range(PEER_TOPK):
        out = jnp.where(idx == a, table[a:a + 1, :], out)
    return out


ROUTE_TOKENS = 256
GATE_PITCH = ROUTE_TOKENS + 8


def _route_gates_kernel(qp_ref, keys_ref, ids_ref, g_ref, sc_ref, ts_ref, ti_ref, bs_ref, bj_ref, cur_ref, prev_ref,
                        gs_ref):
    tm = ROUTE_TOKENS
    K = PEER_TOPK
    groups, _ = _pair_candidates()

    @pl.when(pl.program_id(0) == 0)
    def _():
        prev_ref[...] = jnp.zeros_like(prev_ref)

    for hp in range(2 * PEER_HEADS):
        sc_ref[hp] = _nt(keys_ref[hp], qp_ref[hp])

    sub = lax.broadcasted_iota(jnp.int32, (PEER_NKEYS, PEER_HK), 0).astype(F32)

    def gate_token(t):
        a = prev_ref[0, pl.ds(t, 1), :]
        b = prev_ref[1, pl.ds(t, 1), :]
        w = prev_ref[2, pl.ds(t, 1), :]
        oh1 = jnp.where(sub == a, 1.0, 0.0).astype(BF16)
        oh2 = jnp.where(sub == b, w, 0.0).astype(BF16)
        gs_ref[pl.ds(t, PEER_NKEYS, stride=GATE_PITCH), :] = _nt(oh1, oh2)

    tokens_per_round = tm // (PEER_HEADS * K)

    def stage1(h, carry):
        def side(kk):
            for u in range(tokens_per_round):
                gate_token((h * K + kk) * tokens_per_round + u)

        _extract_topk_grouped([sc_ref.at[2 * h], sc_ref.at[2 * h + 1]], ts_ref, ti_ref,
                              [2 * h * K, (2 * h + 1) * K], side)
        return carry

    lax.fori_loop(0, PEER_HEADS, stage1, 0)

    def candidates(h):
        r0 = pl.multiple_of(2 * h * K, K)
        r1 = pl.multiple_of((2 * h + 1) * K, K)
        s0 = ts_ref[pl.ds(r0, K), :]
        s1 = ts_ref[pl.ds(r1, K), :]
        parts = []
        for kind, fixed, lo, hi in groups:
            if kind == "row0":
                parts.append(s0[fixed:fixed + 1, :] + s1[lo:hi, :])
            elif kind == "col0":
                parts.append(s0[lo:hi, :] + s1[fixed:fixed + 1, :])
            else:
                parts += [s0[a:a + 1, :] + s1[b:b + 1, :] for a, b in fixed]
        return jnp.where(ids_ref[...] >= 0.0, jnp.concatenate(parts, axis=0), -jnp.inf)

    def finish(h, slot):
        r0 = pl.multiple_of(2 * h * K, K)
        r1 = pl.multiple_of((2 * h + 1) * K, K)
        best = bs_ref[slot * K:(slot + 1) * K, :]
        j = bj_ref[slot * K:(slot + 1) * K, :].astype(jnp.int32)
        ex = jnp.exp(best - best[0:1, :])
        out = pl.multiple_of(h * K, K)
        cur_ref[0, pl.ds(out, K), :] = _select_rows(j >> 4, ti_ref[pl.ds(r0, K), :])
        cur_ref[1, pl.ds(out, K), :] = _select_rows(j & (K - 1), ti_ref[pl.ds(r1, K), :])
        cur_ref[2, pl.ds(out, K), :] = ex / jnp.sum(ex, axis=0, keepdims=True)

    slabs_per_round = PEER_NKEYS // (PEER_HEADS // 2 * K)

    def stage2(i, carry):
        def side(kk):
            for u in range(slabs_per_round):
                s = (i * K + kk) * slabs_per_round + u
                g_ref[s] = gs_ref[pl.ds(pl.multiple_of(s * GATE_PITCH, 8), tm), :].astype(BF16)

        ids = ids_ref[...]
        _extract_topk([candidates(2 * i), candidates(2 * i + 1)], jnp.where(ids >= 0.0, ids, 2.0 * PEER_N),
                      bs_ref, bj_ref, [0, K], side)
        finish(2 * i, 0)
        finish(2 * i + 1, 1)
        return carry

    lax.fori_loop(0, PEER_HEADS // 2, stage2, 0)
    for c in range(3):
        prev_ref[c] = jnp.transpose(cur_ref[c])


def _route_gates(qp3, sub_keys):
    T = qp3.shape[1]
    tm = ROUTE_TOKENS
    nblk = T // tm
    keys = sub_keys.reshape(2 * PEER_HEADS, PEER_NKEYS, PEER_DQ // 2).astype(BF16)
    stage1_rows = 2 * PEER_HEADS * PEER_TOPK
    pair_ids = jnp.asarray(np.broadcast_to(_pair_candidates()[1][:, None], (_pair_candidates()[1].shape[0], tm)))
    return pl.pallas_call(
        _route_gates_kernel,
        grid=(nblk + 1,),
        in_specs=[pl.BlockSpec((2 * PEER_HEADS, tm, LANES), lambda i: (0, jnp.minimum(i, nblk - 1), 0)),
                  pl.BlockSpec(keys.shape, lambda i: (0, 0, 0)),
                  pl.BlockSpec(pair_ids.shape, lambda i: (0, 0))],
        out_specs=pl.BlockSpec((PEER_NKEYS, tm, PEER_NKEYS), lambda i: (0, jnp.maximum(i - 1, 0), 0)),
        out_shape=jax.ShapeDtypeStruct((PEER_NKEYS, T, PEER_NKEYS), BF16),
        scratch_shapes=[pltpu.VMEM((2 * PEER_HEADS, PEER_NKEYS, tm), F32),
                        pltpu.VMEM((stage1_rows, tm), F32),
                        pltpu.VMEM((stage1_rows, tm), F32),
                        pltpu.VMEM((2 * PEER_TOPK, tm), F32),
                        pltpu.VMEM((2 * PEER_TOPK, tm), F32),
                        pltpu.VMEM((3, PEER_HK, tm), F32),
                        pltpu.VMEM((3, tm, PEER_HK), F32),
                        pltpu.VMEM((PEER_NKEYS * GATE_PITCH, PEER_NKEYS), F32)],
        compiler_params=_params("arbitrary"),
        name="peer_route_gates",
    )(qp3, keys, pair_ids)


def _peer_kernel(xn_ref, u_ref, g_ref, v_ref, h_ref, gf_ref, o_ref):
    j = pl.program_id(1)

    @pl.when(j == 0)
    def _():
        o_ref[...] = h_ref[...]

    p = _nt(xn_ref[...], u_ref[...].astype(BF16))
    act = 0.5 * p * (1.0 + lax.erf(p * (2.0 ** -0.5)))
    w = jnp.concatenate(
        [(g_ref[s].astype(F32) * act[:, s * LANES:(s + 1) * LANES]).astype(BF16) for s in range(g_ref.shape[0])],
        axis=-1)
    o_ref[...] += _mm(w, v_ref[...].astype(BF16))

    @pl.when(j == pl.num_programs(1) - 1)
    def _():
        o_ref[...] = _rms(o_ref[...], gf_ref[...])


def _peer(xn2, g3, peer_u, peer_v, h2, g_final, tb, nb):
    T = xn2.shape[0]
    ub, vb = peer_u, peer_v
    return pl.pallas_call(
        _peer_kernel,
        grid=(T // tb, PEER_N // nb),
        in_specs=[pl.BlockSpec((tb, D_MODEL), lambda i, j: (i, 0)),
                  pl.BlockSpec((nb, D_MODEL), lambda i, j: (j, 0)),
                  pl.BlockSpec((nb // LANES, tb, LANES), lambda i, j: (j, i, 0)),
                  pl.BlockSpec((nb, D_MODEL), lambda i, j: (j, 0)),
                  pl.BlockSpec((tb, D_MODEL), lambda i, j: (i, 0), pipeline_mode=pl.Buffered(1)),
                  pl.BlockSpec((1, D_MODEL), lambda i, j: (0, 0))],
        out_specs=pl.BlockSpec((tb, D_MODEL), lambda i, j: (i, 0)),
        out_shape=jax.ShapeDtypeStruct((T, D_MODEL), F32),
        compiler_params=_params("parallel", "arbitrary"),
        name="peer_dense",
    )(xn2, ub, g3, vb, h2, g_final.reshape(1, D_MODEL))


def _tile(n, want):
    t = min(n, want)
    assert n % t == 0, (n, t)
    return t


def kernel(x, mem, g_mix, w_in, w_alpha_up, b_alpha, g_gla_out, w_out, g_mem_q, g_mem_kv, w_mq, w_mkv, w_mo,
           g_ffn, w_pq, sub_keys, peer_u, peer_v, g_final):
    B, S, D = x.shape
    T = B * S
    assert D == D_MODEL and S % SB_BLK == 0 and g_mix.shape[0] == 1
    x2 = x.reshape(T, D)

    pg, la, ps = _inproj(x2, g_mix[0], w_in[0], w_alpha_up[0], b_alpha[0], _tile(T, TILE_INPROJ))
    o_gla = _gla(pg.reshape(B, S, -1), la.reshape(B, S, -1), g_gla_out[0])
    o_sb = _stick_breaking(ps.reshape(B, S, -1))
    h1, qm = _outproj(x2, o_gla.reshape(T, -1), o_sb.reshape(T, -1), w_out[0], g_mem_q[0], w_mq[0],
                      _tile(T, TILE_OUTPROJ))
    M = mem.shape[1]
    kv = _memkv(mem.reshape(B * M, D), g_mem_kv[0], w_mkv[0], _tile(B * M, TILE_MEMKV))
    h2, xn, qp = _memattn(h1.reshape(B, S, D), qm.reshape(B, S, D), kv.reshape(B, M, 2 * D),
                          w_mo[0], g_ffn[0], w_pq[0], _tile(S, TILE_MEMATTN))
    gmat = _route_gates(qp.reshape(2 * PEER_HEADS, T, LANES), sub_keys[0])
    out = _peer(xn.reshape(T, D), gmat, peer_u[0], peer_v[0], h2.reshape(T, D), g_final,
                _tile(T, TILE_PEER[0]), TILE_PEER[1])
    return out.reshape(B, S, D)
```
